```python
import math
import jax
import jax.numpy as jnp
from jax import lax
import numpy as np

D_MODEL = 1024
BATCH = 32
SEQ = 256
DEPTH = 2
DEC_BATCH = 2
DEC_SEQ = 1024
PAST_LEN = 512

GRID_W = 64
HEAD_DIM = 64
A_HEADS = D_MODEL // (2 * HEAD_DIM)
A_KV_HEADS = A_HEADS // 4
B_HEADS = D_MODEL // (2 * HEAD_DIM)
NA_ROWS = 8
NA_COLS = 16
C_HEADS = D_MODEL // (2 * HEAD_DIM)
A_Q_W = A_HEADS * HEAD_DIM
A_KV_W = A_KV_HEADS * HEAD_DIM
B_W = B_HEADS * HEAD_DIM
AB_IN = A_Q_W + 2 * A_KV_W + 3 * B_W
AB_OUT = A_Q_W + B_W
C_QK_W = C_HEADS * 2 * HEAD_DIM
C_IN = 3 * C_QK_W
C_OUT = C_HEADS * 2 * HEAD_DIM
N_EXPERTS = 32
TOP_K = 4
D_FF = D_MODEL
SWIGLU_LIMIT = 7.0
SWIGLU_ALPHA = 1.702
ROPE_THETA = 10000.0
Q_BLOCK = 128
RMS_EPS = 1e-6
N_AB_LAYERS = (DEPTH + 1) // 2
N_C_LAYERS = DEPTH // 2

kernel_name = 'hybrid_diffusion_axial_na_diff_moe'


def rmsnorm(x, g):
    xf = x.astype(jnp.float32)
    y = xf * lax.rsqrt(jnp.mean(xf * xf, axis=-1, keepdims=True) + RMS_EPS)
    return (y * g.astype(jnp.float32)).astype(x.dtype)


def modulate(x, g, shift, scale):
    return rmsnorm(x, g) * (1 + scale) + shift


def rope_2d(n_tokens):
    t = jnp.arange(n_tokens)
    row = (t // GRID_W).astype(jnp.float32)
    col = (t % GRID_W).astype(jnp.float32)
    n_freq = HEAD_DIM // 4
    inv_freq = ROPE_THETA ** (-jnp.arange(n_freq, dtype=jnp.float32) / n_freq)
    ang = jnp.concatenate([row[:, None] * inv_freq, col[:, None] * inv_freq], axis=-1)
    return jnp.cos(ang), jnp.sin(ang)


def apply_rope(x, cos, sin):
    shp = x.shape
    xr = x.reshape(shp[:-1] + (shp[-1] // 2, 2)).astype(jnp.float32)
    bshape = (shp[1],) + (1,) * (x.ndim - 3) + (shp[-1] // 2,)
    c = cos.reshape(bshape)
    s = sin.reshape(bshape)
    x0, x1 = xr[..., 0], xr[..., 1]
    out = jnp.stack([x0 * c - x1 * s, x0 * s + x1 * c], axis=-1).reshape(shp)
    return out.astype(x.dtype)


def blocked_attention(q, k, v):
    Bn, N, Hq, dh = q.shape
    Hkv = k.shape[2]
    G = Hq // Hkv
    nb = N // Q_BLOCK
    scale = dh ** -0.5
    qb = q.reshape(Bn, nb, Q_BLOCK, Hkv, G, dh).swapaxes(0, 1)

    def one(q_blk):
        s = jnp.einsum('bqhgd,bkhd->bhgqk', q_blk, k, preferred_element_type=jnp.float32) * scale
        p = jax.nn.softmax(s, axis=-1).astype(v.dtype)
        return jnp.einsum('bhgqk,bkhd->bqhgd', p, v)

    o = lax.map(one, qb)
    return o.swapaxes(0, 1).reshape(Bn, N, Hq * dh)


def neighbourhood_attention(q, k, v, k_ctx, v_ctx, rpb):
    Bn, N, H, dh = q.shape
    rows = N // GRID_W
    wr = min(NA_ROWS, rows)
    scale = dh ** -0.5
    Lc = k_ctx.shape[1]
    q4 = q.reshape(Bn, rows, GRID_W, H, dh).swapaxes(0, 1)
    k4 = k.reshape(Bn, rows, GRID_W, H, dh)
    v4 = v.reshape(Bn, rows, GRID_W, H, dh)
    r_idx = jnp.arange(rows)
    row_start = jnp.clip(r_idx - wr // 2, 0, rows - wr)
    cols = jnp.arange(GRID_W)
    col_start = jnp.clip(cols - NA_COLS // 2, 0, GRID_W - NA_COLS)
    col_valid = (cols[None, :] >= col_start[:, None]) & (cols[None, :] < col_start[:, None] + NA_COLS)
    col_idx = jnp.clip(cols[None, :] - cols[:, None] + NA_COLS - 1, 0, 2 * NA_COLS - 2)
    rpb_cols = rpb[:, :, col_idx]

    def one_row(args):
        q_r, r, rs = args
        k_b = lax.dynamic_slice_in_dim(k4, rs, wr, axis=1).reshape(Bn, wr * GRID_W, H, dh)
        v_b = lax.dynamic_slice_in_dim(v4, rs, wr, axis=1).reshape(Bn, wr * GRID_W, H, dh)
        row_idx = rs + jnp.arange(wr) - r + NA_ROWS - 1
        bias = jnp.where(col_valid, rpb_cols[:, row_idx].astype(jnp.float32), -jnp.inf)
        bias = bias.transpose(0, 2, 1, 3).reshape(H, GRID_W, wr * GRID_W)
        s_loc = jnp.einsum('bqhd,bkhd->bhqk', q_r, k_b, preferred_element_type=jnp.float32) * scale + bias
        s_ctx = jnp.einsum('bqhd,bkhd->bhqk', q_r, k_ctx, preferred_element_type=jnp.float32) * scale
        p = jax.nn.softmax(jnp.concatenate([s_ctx, s_loc], axis=-1), axis=-1).astype(v.dtype)
        return (jnp.einsum('bhqk,bkhd->bqhd', p[..., :Lc], v_ctx)
                + jnp.einsum('bhqk,bkhd->bqhd', p[..., Lc:], v_b))

    o = lax.map(one_row, (q4, r_idx, row_start))
    return o.swapaxes(0, 1).reshape(Bn, N, H * dh)


def blocked_diff_attention(q, k, v, lam):
    Bn, N, H, _, dh = q.shape
    nb = N // Q_BLOCK
    scale = dh ** -0.5
    qb = q.reshape(Bn, nb, Q_BLOCK, H, 2, dh).swapaxes(0, 1)

    def one(q_blk):
        s = jnp.einsum('bqhjd,bkhjd->bhjqk', q_blk, k, preferred_element_type=jnp.float32) * scale
        p = jax.nn.softmax(s, axis=-1)
        w = (p[:, :, 0] - lam * p[:, :, 1]).astype(v.dtype)
        return jnp.einsum('bhqk,bkhe->bqhe', w, v)

    o = lax.map(one, qb)
    return o.swapaxes(0, 1).reshape(Bn, N, H, v.shape[-1])


def ab_project(h, w_in):
    Bn, N, _ = h.shape
    splits = [A_Q_W, A_Q_W + A_KV_W, A_Q_W + 2 * A_KV_W, A_Q_W + 2 * A_KV_W + B_W, A_Q_W + 2 * A_KV_W + 2 * B_W]
    parts = jnp.split(h @ w_in, splits, axis=-1)
    return [t.reshape(Bn, N, -1, HEAD_DIM) for t in parts]


def ab_mixer_context(h, w_in, w_out, q_g, k_g):
    qa, ka, va, qb, kb, vb = ab_project(h, w_in)
    qa, ka = rmsnorm(qa, q_g), rmsnorm(ka, k_g)
    o = jnp.concatenate([blocked_attention(qa, ka, va), blocked_attention(qb, kb, vb)], axis=-1)
    return o @ w_out, (ka, va, kb, vb)


def ab_mixer_latent(h, ka_ctx, va_ctx, kb_ctx, vb_ctx, w_in, w_out, q_g, k_g, rpb):
    qa, ka, va, qb, kb, vb = ab_project(h, w_in)
    cos, sin = rope_2d(h.shape[1])
    qa = apply_rope(rmsnorm(qa, q_g), cos, sin)
    ka = apply_rope(rmsnorm(ka, k_g), cos, sin)
    oa = blocked_attention(qa, jnp.concatenate([ka_ctx, ka], axis=1), jnp.concatenate([va_ctx, va], axis=1))
    ob = neighbourhood_attention(qb, kb, vb, kb_ctx, vb_ctx, rpb)
    return jnp.concatenate([oa, ob], axis=-1) @ w_out


def c_project(h, w_in):
    Bn, N, _ = h.shape
    q, k, v = jnp.split(h @ w_in, [C_QK_W, 2 * C_QK_W], axis=-1)
    return (q.reshape(Bn, N, C_HEADS, 2, HEAD_DIM), k.reshape(Bn, N, C_HEADS, 2, HEAD_DIM),
            v.reshape(Bn, N, C_HEADS, 2 * HEAD_DIM))


def diff_lambda(lq1, lk1, lq2, lk2, lam_init):
    dot_exp = lambda a, b: jnp.exp(jnp.sum(a.astype(jnp.float32) * b.astype(jnp.float32)))
    return dot_exp(lq1, lk1) - dot_exp(lq2, lk2) + lam_init


def c_output(o, subln_g, w_out, lam_init):
    Bn, N = o.shape[:2]
    o = rmsnorm(o, subln_g) * (1 - lam_init)
    return o.reshape(Bn, N, C_OUT) @ w_out


def c_mixer_context(h, w_in, w_out, lam, lam_init, subln_g):
    Bn, N, _ = h.shape
    q, k, v = c_project(h, w_in)
    o = blocked_diff_attention(q, k, v, lam)
    return c_output(o, subln_g, w_out, lam_init), (k.reshape(Bn, N, C_HEADS, 2 * HEAD_DIM), v)


def c_mixer_latent(h, k_ctx, v_ctx, w_in, w_out, lam, lam_init, subln_g):
    Bn, N, _ = h.shape
    q, k, v = c_project(h, w_in)
    cos, sin = rope_2d(N)
    q, k = apply_rope(q, cos, sin), apply_rope(k, cos, sin)
    k_all = jnp.concatenate([k_ctx.reshape(Bn, -1, C_HEADS, 2, HEAD_DIM), k], axis=1)
    v_all = jnp.concatenate([v_ctx, v], axis=1)
    o = blocked_diff_attention(q, k_all, v_all, lam)
    return c_output(o, subln_g, w_out, lam_init)


def moe_ffn(h, router_w, router_b, w_gate_up, b_gate_up, w_down, b_down):
    Bn, N, D = h.shape
    xt = h.reshape(Bn * N, D)
    logits = jnp.dot(xt, router_w, preferred_element_type=jnp.float32) + router_b.astype(jnp.float32)
    top_v, top_i = lax.top_k(logits, TOP_K)
    top_w = jax.nn.softmax(top_v, axis=-1)
    gates = jnp.einsum('tk,tke->te', top_w, jax.nn.one_hot(top_i, N_EXPERTS, dtype=jnp.float32))

    def add_expert(acc, p):
        wgu, bgu, wdn, bdn, g = p
        gu = xt @ wgu + bgu
        gate = jnp.minimum(gu[:, :D_FF], SWIGLU_LIMIT)
        up = jnp.clip(gu[:, D_FF:], -SWIGLU_LIMIT, SWIGLU_LIMIT)
        act = (up + 1) * gate * jax.nn.sigmoid(SWIGLU_ALPHA * gate)
        y = act @ wdn + bdn
        return acc + g[:, None] * y.astype(jnp.float32), None

    acc, _ = lax.scan(add_expert, jnp.zeros((Bn * N, D), jnp.float32),
                      (w_gate_up, b_gate_up, w_down, b_down, gates.T))
    return acc.astype(h.dtype).reshape(Bn, N, D)


def lambda_init(layer):
    return 0.8 - 0.6 * math.exp(-0.3 * layer)


def setup_inputs(seed: int = 0) -> dict:
    key = jax.random.key(seed)
    ks = iter(jax.random.split(key, 40))
    nrm = lambda shape, s=1.0: s * jax.random.normal(next(ks), shape, jnp.float32)
    gain = lambda shape: 1.0 + 0.02 * jax.random.normal(next(ks), shape, jnp.float32)
    return {
        'x_prompt': nrm((BATCH, SEQ, D_MODEL)),
        'x_sample': nrm((DEC_BATCH, DEC_SEQ, D_MODEL)),
        'cache_a_k': nrm((DEC_BATCH, N_AB_LAYERS, PAST_LEN, A_KV_HEADS, HEAD_DIM)),
        'cache_a_v': nrm((DEC_BATCH, N_AB_LAYERS, PAST_LEN, A_KV_HEADS, HEAD_DIM)),
        'cache_b_k': nrm((DEC_BATCH, N_AB_LAYERS, PAST_LEN, B_HEADS, HEAD_DIM)),
        'cache_b_v': nrm((DEC_BATCH, N_AB_LAYERS, PAST_LEN, B_HEADS, HEAD_DIM)),
        'cache_c_k': nrm((DEC_BATCH, N_C_LAYERS, PAST_LEN, C_HEADS, 2 * HEAD_DIM)),
        'cache_c_v': nrm((DEC_BATCH, N_C_LAYERS, PAST_LEN, C_HEADS, 2 * HEAD_DIM)),
        'c': nrm((DEC_BATCH, D_MODEL)),
        'c_ctx': nrm((D_MODEL,)),
        'norm1_g': gain((DEPTH, D_MODEL)),
        'norm2_g': gain((DEPTH, D_MODEL)),
        'w_mod': nrm((DEPTH, D_MODEL, 6 * D_MODEL), 0.5 * D_MODEL ** -0.5),
        'b_mod': nrm((DEPTH, 6 * D_MODEL), 0.01),
        'ab_w_in': nrm((N_AB_LAYERS, D_MODEL, AB_IN), D_MODEL ** -0.5),
        'ab_w_out': nrm((N_AB_LAYERS, AB_OUT, D_MODEL), AB_OUT ** -0.5),
        'a_q_norm_g': gain((N_AB_LAYERS, HEAD_DIM)),
        'a_k_norm_g': gain((N_AB_LAYERS, HEAD_DIM)),
        'b_rpb': nrm((N_AB_LAYERS, B_HEADS, 2 * NA_ROWS - 1, 2 * NA_COLS - 1), 0.1),
        'c_w_in': nrm((N_C_LAYERS, D_MODEL, C_IN), D_MODEL ** -0.5),
        'c_w_out': nrm((N_C_LAYERS, C_OUT, D_MODEL), C_OUT ** -0.5),
        'c_lambda_q1': nrm((N_C_LAYERS, HEAD_DIM), 0.1),
        'c_lambda_k1': nrm((N_C_LAYERS, HEAD_DIM), 0.1),
        'c_lambda_q2': nrm((N_C_LAYERS, HEAD_DIM), 0.1),
        'c_lambda_k2': nrm((N_C_LAYERS, HEAD_DIM), 0.1),
        'c_subln_g': gain((N_C_LAYERS, 2 * HEAD_DIM)),
        'router_w': nrm((DEPTH, D_MODEL, N_EXPERTS), D_MODEL ** -0.5),
        'router_b': nrm((DEPTH, N_EXPERTS), 0.01),
        'w_gate_up': nrm((DEPTH, N_EXPERTS, D_MODEL, 2 * D_FF), D_MODEL ** -0.5),
        'b_gate_up': nrm((DEPTH, N_EXPERTS, 2 * D_FF), 0.01),
        'w_down': nrm((DEPTH, N_EXPERTS, D_FF, D_MODEL), D_FF ** -0.5),
        'b_down': nrm((DEPTH, N_EXPERTS, D_MODEL), 0.01),
        'final_norm_g': gain((D_MODEL,)),
    }


def reference(x_prompt, x_sample, cache_a_k, cache_a_v, cache_b_k, cache_b_v, cache_c_k, cache_c_v,
              c, c_ctx, norm1_g, norm2_g, w_mod, b_mod, ab_w_in, ab_w_out, a_q_norm_g, a_k_norm_g,
              b_rpb, c_w_in, c_w_out, c_lambda_q1, c_lambda_k1, c_lambda_q2, c_lambda_k2, c_subln_g,
              router_w, router_b, w_gate_up, b_gate_up, w_down, b_down, final_norm_g):
    xp = x_prompt
    a_k, a_v, b_k, b_v, c_k, c_v = [], [], [], [], [], []
    for l in range(DEPTH):
        i = l // 2
        mod = jax.nn.silu(c_ctx) @ w_mod[l] + b_mod[l]
        sh1, sc1, g1, sh2, sc2, g2 = jnp.split(mod, 6, axis=-1)
        h = modulate(xp, norm1_g[l], sh1, sc1)
        if l % 2 == 0:
            out, (ka, va, kb, vb) = ab_mixer_context(h, ab_w_in[i], ab_w_out[i], a_q_norm_g[i], a_k_norm_g[i])
            a_k.append(ka)
            a_v.append(va)
            b_k.append(kb)
            b_v.append(vb)
        else:
            lam0 = lambda_init(l)
            lam = diff_lambda(c_lambda_q1[i], c_lambda_k1[i], c_lambda_q2[i], c_lambda_k2[i], lam0)
            out, (kc, vc) = c_mixer_context(h, c_w_in[i], c_w_out[i], lam, lam0, c_subln_g[i])
            c_k.append(kc)
            c_v.append(vc)
        xp = xp + g1 * out
        xp = xp + g2 * moe_ffn(modulate(xp, norm2_g[l], sh2, sc2), router_w[l], router_b[l],
                               w_gate_up[l], b_gate_up[l], w_down[l], b_down[l])
    y_prompt = rmsnorm(xp, final_norm_g)

    xs = x_sample
    for l in range(DEPTH):
        i = l // 2
        mod = (jax.nn.silu(c) @ w_mod[l] + b_mod[l])[:, None, :]
        sh1, sc1, g1, sh2, sc2, g2 = jnp.split(mod, 6, axis=-1)
        h = modulate(xs, norm1_g[l], sh1, sc1)
        if l % 2 == 0:
            out = ab_mixer_latent(h, cache_a_k[:, i], cache_a_v[:, i], cache_b_k[:, i], cache_b_v[:, i],
                                  ab_w_in[i], ab_w_out[i], a_q_norm_g[i], a_k_norm_g[i], b_rpb[i])
        else:
            lam0 = lambda_init(l)
            lam = diff_lambda(c_lambda_q1[i], c_lambda_k1[i], c_lambda_q2[i], c_lambda_k2[i], lam0)
            out = c_mixer_latent(h, cache_c_k[:, i], cache_c_v[:, i], c_w_in[i], c_w_out[i], lam, lam0, c_subln_g[i])
        xs = xs + g1 * out
        xs = xs + g2 * moe_ffn(modulate(xs, norm2_g[l], sh2, sc2), router_w[l], router_b[l],
                               w_gate_up[l], b_gate_up[l], w_down[l], b_down[l])
    y_sample = rmsnorm(xs, final_norm_g)

    new_a_k = jnp.stack(a_k, axis=1)
    new_a_v = jnp.stack(a_v, axis=1)
    new_b_k = jnp.stack(b_k, axis=1)
    new_b_v = jnp.stack(b_v, axis=1)
    new_c_k = jnp.stack(c_k, axis=1)
    new_c_v = jnp.stack(c_v, axis=1)
    return (y_prompt, y_sample, new_a_k, new_a_v, new_b_k, new_b_v, new_c_k, new_c_v)
```

```python
import functools
import math

import jax
import jax.numpy as jnp
from jax import lax
from jax.experimental import pallas as pl
from jax.experimental.pallas import tpu as pltpu

F32 = jnp.float32
BF16 = jnp.bfloat16
I32 = jnp.int32
HIGHEST = lax.Precision.HIGHEST

D_MODEL = 1024
BATCH = 32
SEQ = 256
DEPTH = 2
DEC_BATCH = 2
DEC_SEQ = 1024
PAST_LEN = 512
GRID_W = 64
HEAD_DIM = 64
A_HEADS = 8
A_KV_HEADS = 2
A_GROUP = A_HEADS // A_KV_HEADS
B_HEADS = 8
NA_ROWS = 8
NA_COLS = 16
C_HEADS = 8
A_Q_W = A_HEADS * HEAD_DIM
A_KV_W = A_KV_HEADS * HEAD_DIM
B_W = B_HEADS * HEAD_DIM
C_W = C_HEADS * 2 * HEAD_DIM
N_EXPERTS = 32
TOP_K = 4
D_FF = D_MODEL
SWIGLU_LIMIT = 7.0
SWIGLU_ALPHA = 1.702
ROPE_THETA = 10000.0
RMS_EPS = 1e-6
QK_SCALE = HEAD_DIM ** -0.5

T_CTX = BATCH * SEQ
T_LAT = DEC_BATCH * DEC_SEQ
T_ALL = T_CTX + T_LAT
N_COND = 8
GRID_ROWS = DEC_SEQ // GRID_W
NA_WIN_ROWS = min(NA_ROWS, GRID_ROWS)
NA_LOCAL = NA_WIN_ROWS * GRID_W

LANES = 128
ROW_BLOCK = 256
C_Q_BLOCK = 128
EXPERT_BLOCK = 256
N_SLOTS = T_ALL * TOP_K
N_EXPERT_BLOCKS = N_SLOTS // EXPERT_BLOCK + N_EXPERTS
N_SORTED = N_EXPERT_BLOCKS * EXPERT_BLOCK
DISPATCH_BLOCK = 256
COMBINE_BLOCK = 128
CAST_ROWS = 64
VMEM_LIMIT = 56 * 1024 * 1024


def _params(*sem):
    return pltpu.CompilerParams(dimension_semantics=sem, vmem_limit_bytes=VMEM_LIMIT)


def _cond_of_block(i, block):
    n_ctx = T_CTX // block
    return jnp.where(i < n_ctx, 0, 1 + (i - n_ctx) // (DEC_SEQ // block))


def _rms(x, g):
    ms = jnp.mean(x * x, axis=-1, keepdims=True)
    return x * lax.rsqrt(ms + RMS_EPS) * g


def _modulate(x, g, shift, scale):
    return _rms(x, g) * (1.0 + scale) + shift


def _dot(a, b):
    return jnp.dot(a, b, preferred_element_type=F32)


def _dot_nt(a, b):
    return lax.dot_general(a, b, (((1,), (1,)), ((), ())), preferred_element_type=F32)


def _softmax_parts(scores):
    m = jnp.max(scores[0], axis=-1, keepdims=True)
    for s in scores[1:]:
        m = jnp.maximum(m, jnp.max(s, axis=-1, keepdims=True))
    es = [jnp.exp(s - m) for s in scores]
    tot = jnp.sum(es[0], axis=-1, keepdims=True)
    for e in es[1:]:
        tot = tot + jnp.sum(e, axis=-1, keepdims=True)
    return es, 1.0 / tot


def _attend(scores, values):
    es, inv = _softmax_parts(scores)
    acc = _dot(es[0].astype(BF16), values[0])
    for e, v in zip(es[1:], values[1:]):
        acc = acc + _dot(e.astype(BF16), v)
    return acc * inv


def _rope(x, cosf, sinf):
    lane = lax.broadcasted_iota(I32, x.shape, 1)
    nxt = pltpu.roll(x, LANES - 1, 1)
    prv = pltpu.roll(x, 1, 1)
    return x * cosf + jnp.where((lane & 1) == 0, nxt, prv) * sinf


def _columns(cols, dtype):
    n = cols[0].shape[0]
    lane = lax.broadcasted_iota(I32, (n, len(cols)), 1)
    out = jnp.zeros((n, len(cols)), dtype)
    for k, c in enumerate(cols):
        out = jnp.where(lane == k, c.astype(dtype), out)
    return out


def _mod_kernel(c_ref, w_ref, b_ref, o_ref):
    c = c_ref[...]
    s = c * jax.nn.sigmoid(c)
    o_ref[0] = jnp.dot(s, w_ref[0], precision=HIGHEST, preferred_element_type=F32) + b_ref[0]


def _conditioning(cond, w_mod, b_mod):
    bn = D_MODEL
    n_out = 6 * D_MODEL
    return pl.pallas_call(
        _mod_kernel,
        grid=(DEPTH, n_out // bn),
        in_specs=[
            pl.BlockSpec((N_COND, D_MODEL), lambda l, j: (0, 0)),
            pl.BlockSpec((1, D_MODEL, bn), lambda l, j: (l, 0, j)),
            pl.BlockSpec((1, 1, bn), lambda l, j: (l, 0, j)),
        ],
        out_specs=pl.BlockSpec((1, N_COND, bn), lambda l, j: (l, 0, j)),
        out_shape=jax.ShapeDtypeStruct((DEPTH, N_COND, n_out), F32),
        compiler_params=_params("parallel", "parallel"),
        name="conditioning",
    )(cond, w_mod, b_mod.reshape(DEPTH, 1, n_out))


def _in_proj_kernel(x_ref, mod_ref, g_ref, w_ref, *o_refs):
    h = _modulate(x_ref[...], g_ref[...], mod_ref[0, 0:1, :], mod_ref[0, 1:2, :])
    y = _dot(h.astype(BF16), w_ref[...])
    off = 0
    for o_ref in o_refs:
        w = o_ref.shape[1]
        o_ref[...] = y[:, off:off + w]
        off += w


def _in_proj(x, mod_l, g, w, widths):
    n = w.shape[1]
    assert sum(widths) == n
    return pl.pallas_call(
        _in_proj_kernel,
        grid=(T_ALL // ROW_BLOCK,),
        in_specs=[
            pl.BlockSpec((ROW_BLOCK, D_MODEL), lambda i: (i, 0)),
            pl.BlockSpec((1, 6, D_MODEL), lambda i: (_cond_of_block(i, ROW_BLOCK), 0, 0)),
            pl.BlockSpec((1, D_MODEL), lambda i: (0, 0)),
            pl.BlockSpec((D_MODEL, n), lambda i: (0, 0)),
        ],
        out_specs=[pl.BlockSpec((ROW_BLOCK, wd), lambda i: (i, 0)) for wd in widths],
        out_shape=[jax.ShapeDtypeStruct((T_ALL, wd), F32) for wd in widths],
        compiler_params=_params("parallel"),
        name="in_proj",
    )(x, mod_l, g.reshape(1, D_MODEL), w)


def _attn_ab_ctx_kernel(q_ref, ka_ref, va_ref, kb_ref, vb_ref, qg_ref, kg_ref, o_ref, kan_ref):
    qg = qg_ref[...]
    kg = kg_ref[...]
    kan = [_rms(ka_ref[:, HEAD_DIM * j:HEAD_DIM * (j + 1)], kg) for j in range(A_KV_HEADS)]
    kan_ref[...] = jnp.concatenate(kan, axis=1)
    for j in range(A_KV_HEADS):
        k = kan[j].astype(BF16)
        v = va_ref[:, HEAD_DIM * j:HEAD_DIM * (j + 1)].astype(BF16)
        for h in range(A_GROUP * j, A_GROUP * (j + 1)):
            cols = slice(HEAD_DIM * h, HEAD_DIM * (h + 1))
            q = (_rms(q_ref[:, cols], qg) * QK_SCALE).astype(BF16)
            o_ref[:, cols] = _attend([_dot_nt(q, k)], [v])
    for h in range(B_HEADS):
        cols = slice(HEAD_DIM * h, HEAD_DIM * (h + 1))
        q = (q_ref[:, A_Q_W + HEAD_DIM * h:A_Q_W + HEAD_DIM * (h + 1)] * QK_SCALE).astype(BF16)
        k = kb_ref[:, cols].astype(BF16)
        v = vb_ref[:, cols].astype(BF16)
        o_ref[:, A_Q_W + HEAD_DIM * h:A_Q_W + HEAD_DIM * (h + 1)] = _attend([_dot_nt(q, k)], [v])


def _attn_ab_ctx(q, ka, va, kb, vb, qg, kg):
    seq = lambda w: pl.BlockSpec((SEQ, w), lambda b: (b, 0))
    vec = pl.BlockSpec((1, HEAD_DIM), lambda b: (0, 0))
    return pl.pallas_call(
        _attn_ab_ctx_kernel,
        grid=(BATCH,),
        in_specs=[seq(A_Q_W + B_W), seq(A_KV_W), seq(A_KV_W), seq(B_W), seq(B_W), vec, vec],
        out_specs=[seq(A_Q_W + B_W), seq(A_KV_W)],
        out_shape=[jax.ShapeDtypeStruct((T_CTX, A_Q_W + B_W), F32),
                   jax.ShapeDtypeStruct((T_CTX, A_KV_W), F32)],
        compiler_params=_params("parallel"),
        name="attn_ab_ctx",
    )(q, ka, va, kb, vb, qg.reshape(1, HEAD_DIM), kg.reshape(1, HEAD_DIM))


def _attn_ab_lat_kernel(q_ref, ka_ref, va_ref, kb_ref, vb_ref, cak_ref, cav_ref, cbk_ref, cbv_ref,
                        cos_ref, sin_ref, qg_ref, kg_ref, bias_ref, o_ref, kar_ref):
    r = pl.program_id(1)
    qg = qg_ref[...]
    kg = kg_ref[...]

    @pl.when(r == 0)
    def _():
        kan = jnp.concatenate(
            [_rms(ka_ref[:, HEAD_DIM * j:HEAD_DIM * (j + 1)], kg) for j in range(A_KV_HEADS)], axis=1)
        kar_ref[...] = _rope(kan, cos_ref[...], sin_ref[...]).astype(BF16)

    rows = pl.ds(pl.multiple_of(r * GRID_W, GRID_W), GRID_W)
    cos_q = cos_ref[rows, :]
    sin_q = sin_ref[rows, :]

    qa = []
    for s in range(A_Q_W // LANES):
        pair = jnp.concatenate(
            [_rms(q_ref[:, HEAD_DIM * h:HEAD_DIM * (h + 1)], qg) for h in (2 * s, 2 * s + 1)], axis=1)
        pair = _rope(pair, cos_q, sin_q) * QK_SCALE
        qa += [pair[:, :HEAD_DIM], pair[:, HEAD_DIM:]]
    for j in range(A_KV_HEADS):
        cols = slice(HEAD_DIM * j, HEAD_DIM * (j + 1))
        q = jnp.concatenate(qa[A_GROUP * j:A_GROUP * (j + 1)], axis=0).astype(BF16)
        s_ctx = _dot_nt(q, cak_ref[0, :, cols].astype(BF16))
        s_lat = _dot_nt(q, kar_ref[:, cols])
        o = _attend([s_ctx, s_lat], [cav_ref[0, :, cols].astype(BF16), va_ref[:, cols].astype(BF16)])
        for g in range(A_GROUP):
            h = A_GROUP * j + g
            o_ref[:, HEAD_DIM * h:HEAD_DIM * (h + 1)] = o[GRID_W * g:GRID_W * (g + 1), :]

    rs = jnp.clip(r - NA_WIN_ROWS // 2, 0, GRID_ROWS - NA_WIN_ROWS)
    local = pl.ds(pl.multiple_of(rs * GRID_W, GRID_W), NA_LOCAL)
    for h in range(B_HEADS):
        cols = slice(HEAD_DIM * h, HEAD_DIM * (h + 1))
        q = (q_ref[:, A_Q_W + HEAD_DIM * h:A_Q_W + HEAD_DIM * (h + 1)] * QK_SCALE).astype(BF16)
        s_ctx = _dot_nt(q, cbk_ref[0, :, cols].astype(BF16))
        s_loc = _dot_nt(q, kb_ref[local, cols].astype(BF16)) + bias_ref[0, h]
        o = _attend([s_ctx, s_loc], [cbv_ref[0, :, cols].astype(BF16), vb_ref[local, cols].astype(BF16)])
        o_ref[:, A_Q_W + HEAD_DIM * h:A_Q_W + HEAD_DIM * (h + 1)] = o


def _na_bias_index(r):
    rs = jnp.clip(r - NA_WIN_ROWS // 2, 0, GRID_ROWS - NA_WIN_ROWS)
    return r - rs


def _attn_ab_lat(q, ka, va, kb, vb, cak, cav, cbk, cbv, cos, sin, qg, kg, bias):
    q_blk0 = T_CTX // GRID_W
    kv_blk0 = T_CTX // DEC_SEQ
    qrow = lambda w: pl.BlockSpec((GRID_W, w), lambda b, r: (q_blk0 + b * GRID_ROWS + r, 0))
    kv = lambda w: pl.BlockSpec((DEC_SEQ, w), lambda b, r: (kv_blk0 + b, 0))
    past = lambda w: pl.BlockSpec((1, PAST_LEN, w), lambda b, r: (b, 0, 0))
    full = lambda shape: pl.BlockSpec(shape, lambda b, r: (0,) * len(shape))
    return pl.pallas_call(
        _attn_ab_lat_kernel,
        grid=(DEC_BATCH, GRID_ROWS),
        in_specs=[qrow(A_Q_W + B_W), kv(A_KV_W), kv(A_KV_W), kv(B_W), kv(B_W),
                  past(A_KV_W), past(A_KV_W), past(B_W), past(B_W),
                  full((DEC_SEQ, LANES)), full((DEC_SEQ, LANES)),
                  full((1, HEAD_DIM)), full((1, HEAD_DIM)),
                  pl.BlockSpec((1, B_HEADS, GRID_W, NA_LOCAL), lambda b, r: (_na_bias_index(r), 0, 0, 0))],
        out_specs=pl.BlockSpec((GRID_W, A_Q_W + B_W), lambda b, r: (b * GRID_ROWS + r, 0)),
        out_shape=jax.ShapeDtypeStruct((T_LAT, A_Q_W + B_W), F32),
        scratch_shapes=[pltpu.VMEM((DEC_SEQ, A_KV_W), BF16)],
        compiler_params=_params("parallel", "arbitrary"),
        name="attn_ab_lat",
    )(q, ka, va, kb, vb, cak, cav, cbk, cbv, cos, sin,
      qg.reshape(1, HEAD_DIM), kg.reshape(1, HEAD_DIM), bias)


def _diff_lambda(l_ref, lam_init):
    a = jnp.sum(l_ref[0:1, :] * l_ref[1:2, :], axis=-1, keepdims=True)
    b = jnp.sum(l_ref[2:3, :] * l_ref[3:4, :], axis=-1, keepdims=True)
    return jnp.exp(a) - jnp.exp(b) + lam_init


def _diff_head(q, keys, values, lam, subln_g, lam_init):
    w = None
    for half in range(2):
        cols = slice(HEAD_DIM * half, HEAD_DIM * (half + 1))
        es, inv = _softmax_parts([_dot_nt(q[:, cols], k[:, cols]) for k in keys])
        ps = [e * inv for e in es]
        w = ps if half == 0 else [p0 - lam * p1 for p0, p1 in zip(w, ps)]
    o = _dot(w[0].astype(BF16), values[0])
    for wi, v in zip(w[1:], values[1:]):
        o = o + _dot(wi.astype(BF16), v)
    return _rms(o, subln_g) * (1.0 - lam_init)


def _attn_c_ctx_kernel(q_ref, k_ref, v_ref, l_ref, sg_ref, o_ref, *, lam_init):
    lam = _diff_lambda(l_ref, lam_init)
    sg = sg_ref[...]
    for h in range(C_HEADS):
        cols = slice(2 * HEAD_DIM * h, 2 * HEAD_DIM * (h + 1))
        q = (q_ref[:, cols] * QK_SCALE).astype(BF16)
        o_ref[:, cols] = _diff_head(q, [k_ref[:, cols].astype(BF16)], [v_ref[:, cols].astype(BF16)],
                                    lam, sg, lam_init)


def _attn_c_ctx(q, k, v, lam_vecs, subln_g, lam_init):
    seq = pl.BlockSpec((SEQ, C_W), lambda b: (b, 0))
    return pl.pallas_call(
        functools.partial(_attn_c_ctx_kernel, lam_init=lam_init),
        grid=(BATCH,),
        in_specs=[seq, seq, seq,
                  pl.BlockSpec((4, HEAD_DIM), lambda b: (0, 0)),
                  pl.BlockSpec((1, 2 * HEAD_DIM), lambda b: (0, 0))],
        out_specs=seq,
        out_shape=jax.ShapeDtypeStruct((T_CTX, C_W), F32),
        compiler_params=_params("parallel"),
        name="attn_c_ctx",
    )(q, k, v, lam_vecs, subln_g.reshape(1, 2 * HEAD_DIM))


def _attn_c_lat_kernel(q_ref, k_ref, v_ref, ck_ref, cv_ref, cos_ref, sin_ref, l_ref, sg_ref,
                       o_ref, kr_ref, *, lam_init):
    r = pl.program_id(1)

    @pl.when(r == 0)
    def _():
        cos = cos_ref[...]
        sin = sin_ref[...]
        for s in range(C_W // LANES):
            cols = slice(LANES * s, LANES * (s + 1))
            kr_ref[:, cols] = _rope(k_ref[:, cols], cos, sin).astype(BF16)

    lam = _diff_lambda(l_ref, lam_init)
    sg = sg_ref[...]
    rows = pl.ds(pl.multiple_of(r * C_Q_BLOCK, C_Q_BLOCK), C_Q_BLOCK)
    cos_q = cos_ref[rows, :]
    sin_q = sin_ref[rows, :]
    for h in range(C_HEADS):
        cols = slice(2 * HEAD_DIM * h, 2 * HEAD_DIM * (h + 1))
        q = (_rope(q_ref[:, cols], cos_q, sin_q) * QK_SCALE).astype(BF16)
        keys = [ck_ref[0, :, cols].astype(BF16), kr_ref[:, cols]]
        values = [cv_ref[0, :, cols].astype(BF16), v_ref[:, cols].astype(BF16)]
        o_ref[:, cols] = _diff_head(q, keys, values, lam, sg, lam_init)


def _attn_c_lat(q, k, v, ck, cv, cos, sin, lam_vecs, subln_g, lam_init):
    n_qb = DEC_SEQ // C_Q_BLOCK
    q_blk0 = T_CTX // C_Q_BLOCK
    kv_blk0 = T_CTX // DEC_SEQ
    qrow = pl.BlockSpec((C_Q_BLOCK, C_W), lambda b, r: (q_blk0 + b * n_qb + r, 0))
    kv = pl.BlockSpec((DEC_SEQ, C_W), lambda b, r: (kv_blk0 + b, 0))
    past = pl.BlockSpec((1, PAST_LEN, C_W), lambda b, r: (b, 0, 0))
    full = lambda shape: pl.BlockSpec(shape, lambda b, r: (0,) * len(shape))
    return pl.pallas_call(
        functools.partial(_attn_c_lat_kernel, lam_init=lam_init),
        grid=(DEC_BATCH, n_qb),
        in_specs=[qrow, kv, kv, past, past,
                  full((DEC_SEQ, LANES)), full((DEC_SEQ, LANES)),
                  full((4, HEAD_DIM)), full((1, 2 * HEAD_DIM))],
        out_specs=pl.BlockSpec((C_Q_BLOCK, C_W), lambda b, r: (b * n_qb + r, 0)),
        out_shape=jax.ShapeDtypeStruct((T_LAT, C_W), F32),
        scratch_shapes=[pltpu.VMEM((DEC_SEQ, C_W), BF16)],
        compiler_params=_params("parallel", "arbitrary"),
        name="attn_c_lat",
    )(q, k, v, ck, cv, cos, sin, lam_vecs, subln_g.reshape(1, 2 * HEAD_DIM))


def _out_proj_router_kernel(oc_ref, ol_ref, x_ref, mod_ref, g_ref, w_ref, rw_ref, rb_ref,
                            x1_ref, h2_ref, ti_ref, tw_ref, rk_ref, cnt_ref, carry_ref):
    i = pl.program_id(0)

    @pl.when(i == 0)
    def _():
        carry_ref[...] = jnp.zeros_like(carry_ref)

    o = jnp.where(i < T_CTX // ROW_BLOCK, oc_ref[...], ol_ref[...])
    x1 = x_ref[...] + mod_ref[0, 2:3, :] * _dot(o.astype(BF16), w_ref[...])
    x1_ref[...] = x1
    h2 = _modulate(x1, g_ref[...], mod_ref[0, 3:4, :], mod_ref[0, 4:5, :])
    h2_ref[...] = h2

    logits = jnp.dot(h2, rw_ref[...], precision=HIGHEST, preferred_element_type=F32) + rb_ref[...]
    lane = lax.broadcasted_iota(I32, logits.shape, 1).astype(F32)
    work = logits
    vals, idxs, hots = [], [], []
    for _ in range(TOP_K):
        m = jnp.max(work, axis=-1, keepdims=True)
        idx = jnp.min(jnp.where(work == m, lane, float(N_EXPERTS)), axis=-1, keepdims=True)
        hot = lane == idx
        vals.append(m)
        idxs.append(idx)
        hots.append(hot)
        work = jnp.where(hot, -jnp.inf, work)
    es = [jnp.exp(v - vals[0]) for v in vals]
    inv = 1.0 / (es[0] + es[1] + es[2] + es[3])
    ti_ref[...] = _columns(idxs, I32)
    tw_ref[...] = _columns([e * inv for e in es], F32)

    chosen = jnp.zeros(logits.shape, F32)
    for hot in hots:
        chosen = chosen + hot.astype(F32)
    n = logits.shape[0]
    earlier = (lax.broadcasted_iota(I32, (n, n), 0) > lax.broadcasted_iota(I32, (n, n), 1))
    before = _dot(earlier.astype(BF16), chosen.astype(BF16)) + carry_ref[...]
    ranks = [jnp.sum(jnp.where(hot, before, 0.0), axis=-1, keepdims=True) for hot in hots]
    rk_ref[...] = _columns(ranks, I32)
    carry_ref[...] = carry_ref[...] + jnp.sum(chosen, axis=0, keepdims=True)
    cnt_ref[...] = carry_ref[...].astype(I32)


def _out_proj_router(o_ctx, o_lat, x, mod_l, g, w, router_w, router_b):
    n_ctx = T_CTX // ROW_BLOCK
    row = lambda w_: pl.BlockSpec((ROW_BLOCK, w_), lambda i: (i, 0))
    full = lambda shape: pl.BlockSpec(shape, lambda i: (0,) * len(shape))
    return pl.pallas_call(
        _out_proj_router_kernel,
        grid=(T_ALL // ROW_BLOCK,),
        in_specs=[pl.BlockSpec((ROW_BLOCK, D_MODEL), lambda i: (jnp.minimum(i, n_ctx - 1), 0)),
                  pl.BlockSpec((ROW_BLOCK, D_MODEL), lambda i: (jnp.maximum(i - n_ctx, 0), 0)),
                  row(D_MODEL),
                  pl.BlockSpec((1, 6, D_MODEL), lambda i: (_cond_of_block(i, ROW_BLOCK), 0, 0)),
                  full((1, D_MODEL)), full((D_MODEL, D_MODEL)),
                  full((D_MODEL, N_EXPERTS)), full((1, N_EXPERTS))],
        out_specs=[row(D_MODEL), row(D_MODEL), row(TOP_K), row(TOP_K), row(TOP_K), full((1, N_EXPERTS))],
        out_shape=[jax.ShapeDtypeStruct((T_ALL, D_MODEL), F32),
                   jax.ShapeDtypeStruct((T_ALL, D_MODEL), F32),
                   jax.ShapeDtypeStruct((T_ALL, TOP_K), I32),
                   jax.ShapeDtypeStruct((T_ALL, TOP_K), F32),
                   jax.ShapeDtypeStruct((T_ALL, TOP_K), I32),
                   jax.ShapeDtypeStruct((1, N_EXPERTS), I32)],
        scratch_shapes=[pltpu.VMEM((1, N_EXPERTS), F32)],
        compiler_params=_params("arbitrary"),
        name="out_proj_router",
    )(o_ctx, o_lat, x, mod_l, g.reshape(1, D_MODEL), w, router_w, router_b.reshape(1, N_EXPERTS))


def _routing_tables(top_i, rank, counts):
    cnt = counts[0]
    nblk = (cnt + EXPERT_BLOCK - 1) // EXPERT_BLOCK
    cum = jnp.cumsum(nblk)
    start = cum - nblk
    experts = jnp.arange(N_EXPERTS, dtype=I32)
    start_of = jnp.sum(jnp.where(top_i[..., None] == experts, start, 0), axis=-1)
    pos = (start_of * EXPERT_BLOCK + rank).reshape(-1).astype(I32)
    n_used = cum[-1].astype(I32)
    blocks = jnp.arange(N_EXPERT_BLOCKS, dtype=I32)
    owner = jnp.sum((blocks[:, None] >= cum[None, :]).astype(I32), axis=1)
    last_owner = jnp.sum((n_used - 1 >= cum).astype(I32))
    owner = jnp.where(blocks < n_used, owner, last_owner).astype(I32)
    spare = n_used + experts
    clear = jnp.concatenate([jnp.where(nblk > 0, start + nblk - 1, -1),
                             jnp.where(spare < N_EXPERT_BLOCKS, spare, -1)]).astype(I32)
    return pos, owner, n_used.reshape(1), clear


def _dispatch_kernel(pos_ref, clear_ref, h_ref, xs_ref, zero_ref, sem, zsem):
    i = pl.program_id(0)

    @pl.when(i == 0)
    def _():
        zero_ref[...] = jnp.zeros_like(zero_ref)

        def clear(n):
            rows = pl.ds(pl.multiple_of(clear_ref[n] * EXPERT_BLOCK, EXPERT_BLOCK), EXPERT_BLOCK)
            return pltpu.make_async_copy(zero_ref, xs_ref.at[rows, :], zsem)

        def start(n, c):
            @pl.when(clear_ref[n] >= 0)
            def _():
                clear(n).start()
            return c

        def wait(n, c):
            @pl.when(clear_ref[n] >= 0)
            def _():
                clear(n).wait()
            return c

        lax.fori_loop(0, 2 * N_EXPERTS, start, 0)
        lax.fori_loop(0, 2 * N_EXPERTS, wait, 0)

    base = i * (DISPATCH_BLOCK * TOP_K)

    def scatter(t, c):
        for k in range(TOP_K):
            p = pos_ref[base + t * TOP_K + k]
            pltpu.make_async_copy(h_ref.at[pl.ds(t, 1), :], xs_ref.at[pl.ds(p, 1), :], sem).start()
        return c

    lax.fori_loop(0, DISPATCH_BLOCK, scatter, 0)
    for _ in range(TOP_K):
        pltpu.make_async_copy(h_ref, xs_ref.at[pl.ds(0, DISPATCH_BLOCK), :], sem).wait()


def _dispatch(pos, clear, h2):
    return pl.pallas_call(
        _dispatch_kernel,
        grid_spec=pltpu.PrefetchScalarGridSpec(
            num_scalar_prefetch=2,
            grid=(T_ALL // DISPATCH_BLOCK,),
            in_specs=[pl.BlockSpec((DISPATCH_BLOCK, D_MODEL), lambda i, *_: (i, 0))],
            out_specs=pl.BlockSpec(memory_space=pl.ANY),
            scratch_shapes=[pltpu.VMEM((EXPERT_BLOCK, D_MODEL), F32),
                            pltpu.SemaphoreType.DMA(()), pltpu.SemaphoreType.DMA(())],
        ),
        out_shape=jax.ShapeDtypeStruct((N_SORTED, D_MODEL), F32),
        compiler_params=_params("arbitrary"),
        name="dispatch",
    )(pos, clear, h2)


def _experts_kernel(owner_ref, used_ref, xs_ref, wgu_ref, bgu_ref, wdn_ref, bdn_ref, ys_ref, wgu_b, wdn_b):
    i = pl.program_id(0)
    fresh = jnp.logical_or(i == 0, owner_ref[i] != owner_ref[jnp.maximum(i - 1, 0)])

    @pl.when(fresh)
    def _():
        def cast(c, carry):
            rows = pl.ds(pl.multiple_of(c * CAST_ROWS, CAST_ROWS), CAST_ROWS)
            wgu_b[rows, :] = wgu_ref[0, rows, :].astype(BF16)
            wdn_b[rows, :] = wdn_ref[0, rows, :].astype(BF16)
            return carry

        lax.fori_loop(0, D_MODEL // CAST_ROWS, cast, 0)

    @pl.when(i < used_ref[0])
    def _():
        gu = _dot(xs_ref[...].astype(BF16), wgu_b[...]) + bgu_ref[0]
        gate = jnp.minimum(gu[:, :D_FF], SWIGLU_LIMIT)
        up = jnp.clip(gu[:, D_FF:], -SWIGLU_LIMIT, SWIGLU_LIMIT)
        act = (up + 1.0) * gate * jax.nn.sigmoid(SWIGLU_ALPHA * gate)
        ys_ref[...] = _dot(act.astype(BF16), wdn_b[...]) + bdn_ref[0]

    @pl.when(i >= used_ref[0])
    def _():
        ys_ref[...] = jnp.zeros_like(ys_ref)


def _experts(owner, n_used, xs, wgu, bgu, wdn, bdn):
    return pl.pallas_call(
        _experts_kernel,
        grid_spec=pltpu.PrefetchScalarGridSpec(
            num_scalar_prefetch=2,
            grid=(N_EXPERT_BLOCKS,),
            in_specs=[
                pl.BlockSpec((EXPERT_BLOCK, D_MODEL), lambda i, own, used: (jnp.minimum(i, used[0] - 1), 0)),
                pl.BlockSpec((1, D_MODEL, 2 * D_FF), lambda i, own, used: (own[i], 0, 0)),
                pl.BlockSpec((1, 1, 2 * D_FF), lambda i, own, used: (own[i], 0, 0)),
                pl.BlockSpec((1, D_FF, D_MODEL), lambda i, own, used: (own[i], 0, 0)),
                pl.BlockSpec((1, 1, D_MODEL), lambda i, own, used: (own[i], 0, 0)),
            ],
            out_specs=pl.BlockSpec((EXPERT_BLOCK, D_MODEL), lambda i, own, used: (i, 0)),
            scratch_shapes=[pltpu.VMEM((D_MODEL, 2 * D_FF), BF16), pltpu.VMEM((D_FF, D_MODEL), BF16)],
        ),
        out_shape=jax.ShapeDtypeStruct((N_SORTED, D_MODEL), F32),
        compiler_params=_params("arbitrary"),
        name="experts",
    )(owner, n_used, xs, wgu, bgu.reshape(N_EXPERTS, 1, 2 * D_FF), wdn, bdn.reshape(N_EXPERTS, 1, D_MODEL))


def _combine_kernel(pos_ref, ys_ref, x_ref, tw_ref, mod_ref, g_ref, o_ref, buf, sem, *, final_norm):
    base = pl.program_id(0) * (COMBINE_BLOCK * TOP_K)

    def gather(t, c):
        for k in range(TOP_K):
            p = pos_ref[base + t * TOP_K + k]
            pltpu.make_async_copy(ys_ref.at[pl.ds(p, 1), :], buf.at[k, pl.ds(t, 1), :], sem).start()
        return c

    lax.fori_loop(0, COMBINE_BLOCK, gather, 0)
    for k in range(TOP_K):
        pltpu.make_async_copy(ys_ref.at[pl.ds(0, COMBINE_BLOCK), :], buf.at[k], sem).wait()

    tw = tw_ref[...]
    acc = tw[:, 0:1] * buf[0]
    for k in range(1, TOP_K):
        acc = acc + tw[:, k:k + 1] * buf[k]
    x2 = x_ref[...] + mod_ref[0, 5:6, :] * acc
    o_ref[...] = _rms(x2, g_ref[...]) if final_norm else x2


def _combine(pos, ys, x1, top_w, mod_l, final_g, final_norm):
    return pl.pallas_call(
        functools.partial(_combine_kernel, final_norm=final_norm),
        grid_spec=pltpu.PrefetchScalarGridSpec(
            num_scalar_prefetch=1,
            grid=(T_ALL // COMBINE_BLOCK,),
            in_specs=[
                pl.BlockSpec(memory_space=pl.ANY),
                pl.BlockSpec((COMBINE_BLOCK, D_MODEL), lambda i, *_: (i, 0)),
                pl.BlockSpec((COMBINE_BLOCK, TOP_K), lambda i, *_: (i, 0)),
                pl.BlockSpec((1, 6, D_MODEL), lambda i, *_: (_cond_of_block(i, COMBINE_BLOCK), 0, 0)),
                pl.BlockSpec((1, D_MODEL), lambda i, *_: (0, 0)),
            ],
            out_specs=pl.BlockSpec((COMBINE_BLOCK, D_MODEL), lambda i, *_: (i, 0)),
            scratch_shapes=[pltpu.VMEM((TOP_K, COMBINE_BLOCK, D_MODEL), F32), pltpu.SemaphoreType.DMA(())],
        ),
        out_shape=jax.ShapeDtypeStruct((T_ALL, D_MODEL), F32),
        compiler_params=_params("arbitrary"),
        name="combine",
    )(pos, ys, x1, top_w, mod_l, final_g.reshape(1, D_MODEL))


def _moe(x1, h2, top_i, top_w, rank, counts, mod_l, wgu, bgu, wdn, bdn, final_g, final_norm):
    pos, owner, n_used, clear = _routing_tables(top_i, rank, counts)
    xs = _dispatch(pos, clear, h2)
    ys = _experts(owner, n_used, xs, wgu, bgu, wdn, bdn)
    return _combine(pos, ys, x1, top_w, mod_l, final_g, final_norm)


def _rope_tables():
    t = jnp.arange(DEC_SEQ)
    row = (t // GRID_W).astype(F32)
    col = (t % GRID_W).astype(F32)
    n_freq = HEAD_DIM // 4
    inv_freq = ROPE_THETA ** (-jnp.arange(n_freq, dtype=F32) / n_freq)
    ang = jnp.concatenate([row[:, None] * inv_freq, col[:, None] * inv_freq], axis=-1)
    cos = jnp.repeat(jnp.cos(ang), 2, axis=-1)
    sin = jnp.repeat(jnp.sin(ang), 2, axis=-1) * jnp.tile(jnp.array([-1.0, 1.0], F32), HEAD_DIM // 2)
    reps = LANES // HEAD_DIM
    return jnp.tile(cos, (1, reps)), jnp.tile(sin, (1, reps))


def _na_bias_table(rpb):
    cols = jnp.arange(GRID_W)
    col_start = jnp.clip(cols - NA_COLS // 2, 0, GRID_W - NA_COLS)
    valid = (cols[None, :] >= col_start[:, None]) & (cols[None, :] < col_start[:, None] + NA_COLS)
    col_idx = jnp.clip(cols[None, :] - cols[:, None] + NA_COLS - 1, 0, 2 * NA_COLS - 2)
    by_col = jnp.where(valid, rpb[:, :, col_idx], -jnp.inf)
    off = jnp.arange(NA_WIN_ROWS)[:, None]
    j = jnp.arange(NA_WIN_ROWS)[None, :]
    row_idx = j - off + NA_ROWS - 1
    tab = by_col[:, row_idx]
    tab = tab.transpose(1, 0, 3, 2, 4)
    return tab.reshape(NA_WIN_ROWS, B_HEADS, GRID_W, NA_LOCAL).astype(F32)


def _lambda_init(layer):
    return 0.8 - 0.6 * math.exp(-0.3 * layer)


def kernel(x_prompt, x_sample, cache_a_k, cache_a_v, cache_b_k, cache_b_v, cache_c_k, cache_c_v, c, c_ctx, norm1_g, norm2_g, w_mod, b_mod, ab_w_in, ab_w_out, a_q_norm_g, a_k_norm_g, b_rpb, c_w_in, c_w_out, c_lambda_q1, c_lambda_k1, c_lambda_q2, c_lambda_k2, c_subln_g, router_w, router_b, w_gate_up, b_gate_up, w_down, b_down, final_norm_g):
    x = jnp.concatenate([x_prompt.reshape(T_CTX, D_MODEL), x_sample.reshape(T_LAT, D_MODEL)], axis=0)
    cond = jnp.concatenate([c_ctx[None, :], c, jnp.zeros((N_COND - 1 - DEC_BATCH, D_MODEL), F32)], axis=0)
    mod = _conditioning(cond, w_mod, b_mod).reshape(DEPTH, N_COND, 6, D_MODEL)
    cos, sin = _rope_tables()

    new = {}
    for l in range(DEPTH):
        i = l // 2
        if l % 2 == 0:
            w = ab_w_in[i]
            q0, ka0, va0, qb0, kb0, vb0 = 0, A_Q_W, A_Q_W + A_KV_W, A_Q_W + 2 * A_KV_W, \
                A_Q_W + 2 * A_KV_W + B_W, A_Q_W + 2 * A_KV_W + 2 * B_W
            w = jnp.concatenate([w[:, q0:ka0], w[:, qb0:kb0], w[:, ka0:va0], w[:, va0:qb0],
                                 w[:, kb0:vb0], w[:, vb0:]], axis=1).astype(BF16)
            q, ka, va, kb, vb = _in_proj(x, mod[l], norm1_g[l], w,
                                         (A_Q_W + B_W, A_KV_W, A_KV_W, B_W, B_W))
            o_ctx, ka_n = _attn_ab_ctx(q, ka, va, kb, vb, a_q_norm_g[i], a_k_norm_g[i])
            o_lat = _attn_ab_lat(q, ka, va, kb, vb,
                             cache_a_k[:, i].reshape(DEC_BATCH, PAST_LEN, A_KV_W),
                             cache_a_v[:, i].reshape(DEC_BATCH, PAST_LEN, A_KV_W),
                             cache_b_k[:, i].reshape(DEC_BATCH, PAST_LEN, B_W),
                             cache_b_v[:, i].reshape(DEC_BATCH, PAST_LEN, B_W),
                             cos, sin, a_q_norm_g[i], a_k_norm_g[i], _na_bias_table(b_rpb[i]))
            new.setdefault("a_k", []).append(ka_n.reshape(BATCH, SEQ, A_KV_HEADS, HEAD_DIM))
            new.setdefault("a_v", []).append(va[:T_CTX].reshape(BATCH, SEQ, A_KV_HEADS, HEAD_DIM))
            new.setdefault("b_k", []).append(kb[:T_CTX].reshape(BATCH, SEQ, B_HEADS, HEAD_DIM))
            new.setdefault("b_v", []).append(vb[:T_CTX].reshape(BATCH, SEQ, B_HEADS, HEAD_DIM))
            w_out = ab_w_out[i]
        else:
            lam0 = _lambda_init(l)
            lam_vecs = jnp.stack([c_lambda_q1[i], c_lambda_k1[i], c_lambda_q2[i], c_lambda_k2[i]])
            q, k, v = _in_proj(x, mod[l], norm1_g[l], c_w_in[i].astype(BF16), (C_W, C_W, C_W))
            o_ctx = _attn_c_ctx(q, k, v, lam_vecs, c_subln_g[i], lam0)
            o_lat = _attn_c_lat(q, k, v,
                            cache_c_k[:, i].reshape(DEC_BATCH, PAST_LEN, C_W),
                            cache_c_v[:, i].reshape(DEC_BATCH, PAST_LEN, C_W),
                            cos, sin, lam_vecs, c_subln_g[i], lam0)
            new.setdefault("c_k", []).append(k[:T_CTX].reshape(BATCH, SEQ, C_HEADS, 2 * HEAD_DIM))
            new.setdefault("c_v", []).append(v[:T_CTX].reshape(BATCH, SEQ, C_HEADS, 2 * HEAD_DIM))
            w_out = c_w_out[i]
        x1, h2, top_i, top_w, rank, counts = _out_proj_router(
            o_ctx, o_lat, x, mod[l], norm2_g[l], w_out.astype(BF16), router_w[l], router_b[l])
        x = _moe(x1, h2, top_i, top_w, rank, counts, mod[l], w_gate_up[l], b_gate_up[l],
                 w_down[l], b_down[l], final_norm_g, final_norm=(l == DEPTH - 1))

    y_prompt = x[:T_CTX].reshape(BATCH, SEQ, D_MODEL)
    y_sample = x[T_CTX:].reshape(DEC_BATCH, DEC_SEQ, D_MODEL)
    stack = lambda name: jnp.stack(new[name], axis=1)
    return (y_prompt, y_sample, stack("a_k"), stack("a_v"), stack("b_k"), stack("b_v"),
            stack("c_k"), stack("c_v"))
```
